```python
import jax, jax.numpy as jnp
from jax import lax
import numpy as np

D_MODEL = 1024
BATCH = 4
SEQ = 8192
DEPTH = 2

GRID_W = 64
CTX_LEN = 256
CONV_WIDTH = D_MODEL
CONV_KERNEL = 31
N_HEADS = 16
N_KV_HEADS = 4
HEAD_DIM = 64
GROUP = N_HEADS // N_KV_HEADS
ATTN_WIDTH = N_HEADS * HEAD_DIM
KV_WIDTH = N_KV_HEADS * HEAD_DIM
Q_BLOCK = 128
ROPE_THETA = 10000.0
ROPE_AXIS_DIM = HEAD_DIM // 2
EPS = 1e-6
ATTN_SCALE = HEAD_DIM ** -0.5
IN_SPLITS = (2 * CONV_WIDTH, CONV_WIDTH, ATTN_WIDTH, KV_WIDTH, KV_WIDTH, ATTN_WIDTH, 2 * D_MODEL)
IN_WIDTH = sum(IN_SPLITS)
IN_OFFSETS = tuple(int(o) for o in np.cumsum(IN_SPLITS)[:-1])

kernel_name = "hybrid_conformer_gqa_prefix_dit_block"


def rms_norm(x, g):
    xf = x.astype(jnp.float32)
    y = xf * lax.rsqrt(jnp.mean(xf * xf, axis=-1, keepdims=True) + EPS)
    return (y * g.astype(jnp.float32)).astype(x.dtype)


def layer_norm(x, g, b):
    xf = x.astype(jnp.float32)
    mu = jnp.mean(xf, axis=-1, keepdims=True)
    var = jnp.mean(jnp.square(xf - mu), axis=-1, keepdims=True)
    y = (xf - mu) * lax.rsqrt(var + EPS)
    return (y * g.astype(jnp.float32) + b.astype(jnp.float32)).astype(x.dtype)


def axial_rope_tables(n_tokens):
    rows = n_tokens // GRID_W
    row = jnp.repeat(jnp.arange(rows, dtype=jnp.float32), GRID_W)
    col = jnp.tile(jnp.arange(GRID_W, dtype=jnp.float32), rows)
    inv_freq = ROPE_THETA ** (-jnp.arange(0, ROPE_AXIS_DIM, 2, dtype=jnp.float32) / ROPE_AXIS_DIM)
    ang = jnp.concatenate([row[:, None] * inv_freq, col[:, None] * inv_freq], axis=-1)
    return jnp.cos(ang), jnp.sin(ang)


def apply_rope(x, cos, sin):
    cos = cos.astype(x.dtype)[None, :, None, :]
    sin = sin.astype(x.dtype)[None, :, None, :]
    x1, x2 = jnp.split(x, 2, axis=-1)
    return jnp.concatenate([x1 * cos - x2 * sin, x2 * cos + x1 * sin], axis=-1)


def gqa_attend(qblk, k_all, v_all):
    s = jnp.einsum('bqkgd,bskd->bkgqs', qblk, k_all).astype(jnp.float32) * ATTN_SCALE
    p = jax.nn.softmax(s, axis=-1).astype(v_all.dtype)
    return jnp.einsum('bkgqs,bskd->bqkgd', p, v_all)


def latent_attention(q, k_all, v_all):
    b, n = q.shape[0], q.shape[1]
    nblk = n // Q_BLOCK
    qb = q.reshape(b, nblk, Q_BLOCK, N_KV_HEADS, GROUP, HEAD_DIM).transpose(1, 0, 2, 3, 4, 5)
    o = lax.map(lambda qblk: gqa_attend(qblk, k_all, v_all), qb)
    return o.transpose(1, 0, 2, 3, 4, 5).reshape(b, n, ATTN_WIDTH)


def conv_module(u, gate, conv_w, conv_b, ln_g, ln_b, w_conv_out):
    a, g = jnp.split(u, 2, axis=-1)
    y = a * jax.nn.sigmoid(g)
    pad = CONV_KERNEL // 2
    y = lax.conv_general_dilated(
        y, conv_w[:, None, :], window_strides=(1,), padding=[(pad, pad)],
        dimension_numbers=('NWC', 'WIO', 'NWC'), feature_group_count=CONV_WIDTH) + conv_b
    y = jax.nn.silu(layer_norm(y, ln_g, ln_b))
    y = y * jax.nn.silu(gate)
    return y @ w_conv_out


def split_proj(p):
    return jnp.split(p, IN_OFFSETS, axis=-1)


def merge_branches(y_conv, y_attn, gm, w_out):
    ga, gb = jnp.split(gm, 2, axis=-1)
    return (jax.nn.sigmoid(ga) * y_conv + jax.nn.sigmoid(gb) * y_attn) @ w_out


def setup_inputs(seed: int = 0) -> dict:
    key = jax.random.key(seed)
    ks = jax.random.split(key, 20)
    f32 = jnp.float32

    def nrm(k, shape, scale):
        return jax.random.normal(k, shape, f32) * scale

    L, D = DEPTH, D_MODEL
    return {
        "x": nrm(ks[0], (BATCH, SEQ, D), 1.0),
        "c": nrm(ks[1], (BATCH, D), 1.0),
        "ctx": nrm(ks[2], (BATCH, CTX_LEN, D), 1.0),
        "c_ctx": nrm(ks[3], (D,), 1.0),
        "w_mod": nrm(ks[4], (L, D, 3 * D), 0.5 * D ** -0.5),
        "b_mod": nrm(ks[5], (L, 3 * D), 0.02),
        "g_pre": 1.0 + nrm(ks[6], (L, D), 0.02),
        "g_post": 1.0 + nrm(ks[7], (L, D), 0.02),
        "w_in": nrm(ks[8], (L, D, IN_WIDTH), D ** -0.5),
        "conv_w": nrm(ks[9], (L, CONV_KERNEL, CONV_WIDTH), CONV_KERNEL ** -0.5),
        "conv_b": nrm(ks[10], (L, CONV_WIDTH), 0.02),
        "ln_g": 1.0 + nrm(ks[11], (L, CONV_WIDTH), 0.02),
        "ln_b": nrm(ks[12], (L, CONV_WIDTH), 0.02),
        "w_conv_out": nrm(ks[13], (L, CONV_WIDTH, D), CONV_WIDTH ** -0.5),
        "q_norm_g": 1.0 + nrm(ks[14], (L, HEAD_DIM), 0.02),
        "k_norm_g": 1.0 + nrm(ks[15], (L, HEAD_DIM), 0.02),
        "w_attn_out": nrm(ks[16], (L, ATTN_WIDTH, D), ATTN_WIDTH ** -0.5),
        "w_out": nrm(ks[17], (L, D, D), D ** -0.5),
    }


def reference(x, c, ctx, c_ctx, w_mod, b_mod, g_pre, g_post, w_in, conv_w, conv_b,
              ln_g, ln_b, w_conv_out, q_norm_g, k_norm_g, w_attn_out, w_out):
    b, n, _ = x.shape
    cos, sin = axial_rope_tables(n)

    for l in range(DEPTH):
        last = l == DEPTH - 1
        sh, sc, gt = jnp.split(jax.nn.silu(c) @ w_mod[l] + b_mod[l], 3, axis=-1)
        shc, scc, gtc = jnp.split(jax.nn.silu(c_ctx) @ w_mod[l] + b_mod[l], 3, axis=-1)

        h = rms_norm(x, g_pre[l]) * (1.0 + sc[:, None, :]) + sh[:, None, :]
        hc = rms_norm(ctx, g_pre[l]) * (1.0 + scc) + shc

        ua, gate_a, q, k, v, gate_b, gm = split_proj(h @ w_in[l])
        ua_c, gate_a_c, q_c, k_c, v_c, gate_b_c, gm_c = split_proj(hc @ w_in[l])

        q = apply_rope(rms_norm(q.reshape(b, n, N_HEADS, HEAD_DIM), q_norm_g[l]), cos, sin)
        k = apply_rope(rms_norm(k.reshape(b, n, N_KV_HEADS, HEAD_DIM), k_norm_g[l]), cos, sin)
        v = v.reshape(b, n, N_KV_HEADS, HEAD_DIM)
        k_c = rms_norm(k_c.reshape(b, CTX_LEN, N_KV_HEADS, HEAD_DIM), k_norm_g[l])
        v_c = v_c.reshape(b, CTX_LEN, N_KV_HEADS, HEAD_DIM)
        k_all = jnp.concatenate([k, k_c], axis=1)
        v_all = jnp.concatenate([v, v_c], axis=1)
        o = latent_attention(q, k_all, v_all)
        y_attn = (o * jax.nn.silu(gate_b)) @ w_attn_out[l]

        y_conv = conv_module(ua, gate_a, conv_w[l], conv_b[l], ln_g[l], ln_b[l], w_conv_out[l])

        out = merge_branches(y_conv, y_attn, gm, w_out[l])
        x_new = x + gt[:, None, :] * rms_norm(out, g_post[l])

        if not last:
            q_c = rms_norm(q_c.reshape(b, CTX_LEN, N_KV_HEADS, GROUP, HEAD_DIM), q_norm_g[l])
            o_c = gqa_attend(q_c, k_c, v_c).reshape(b, CTX_LEN, ATTN_WIDTH)
            y_attn_c = (o_c * jax.nn.silu(gate_b_c)) @ w_attn_out[l]
            y_conv_c = conv_module(ua_c, gate_a_c, conv_w[l], conv_b[l], ln_g[l], ln_b[l], w_conv_out[l])
            out_c = merge_branches(y_conv_c, y_attn_c, gm_c, w_out[l])
            ctx = ctx + gtc * rms_norm(out_c, g_post[l])
        x = x_new

    return x
```

```python
import functools
import math

import jax
import jax.numpy as jnp
from jax import lax
from jax.experimental import pallas as pl
from jax.experimental.pallas import tpu as pltpu

D_MODEL = 1024
GRID_W = 64
CONV_KERNEL = 31
CONV_PAD = CONV_KERNEL // 2
N_HEADS = 16
N_KV_HEADS = 4
HEAD_DIM = 64
GROUP = N_HEADS // N_KV_HEADS
KV_WIDTH = N_KV_HEADS * HEAD_DIM
ROPE_THETA = 10000.0
ROPE_AXIS_DIM = HEAD_DIM // 2
EPS = 1e-6
ATTN_SCALE = HEAD_DIM ** -0.5
LOG2E = math.log2(math.e)

V7X_LANES = 128
V7X_VMEM_LIMIT_BYTES = 56 * 1024 * 1024

ROW_TILE = 256
HALO_ROWS = 16
NEG_BIG = -1e30

F32 = jnp.float32
BF16 = jnp.bfloat16


def _sigmoid(x):
    return 1.0 / (1.0 + jnp.exp(-x))


def _const_spec(shape):
    zeros = (0,) * len(shape)
    return pl.BlockSpec(shape, lambda *_: zeros, pipeline_mode=pl.Buffered(1))


def _mod_kernel(cs_ref, w_ref, b_ref, o_ref):
    c = cs_ref[...]
    a = c * _sigmoid(c)
    w = w_ref[0]
    a_hi = a.astype(BF16)
    a_lo = (a - a_hi.astype(F32)).astype(BF16)
    w_hi = w.astype(BF16)
    w_lo = (w - w_hi.astype(F32)).astype(BF16)
    acc = jnp.dot(a_hi, w_hi, preferred_element_type=F32)
    acc += jnp.dot(a_hi, w_lo, preferred_element_type=F32)
    acc += jnp.dot(a_lo, w_hi, preferred_element_type=F32)
    o_ref[0] = acc + b_ref[0]


def _modulation(cs, w_mod, b_mod):
    depth, d, d3 = w_mod.shape
    rows = cs.shape[0]
    col = 1024
    return pl.pallas_call(
        _mod_kernel,
        out_shape=jax.ShapeDtypeStruct((depth, rows, d3), F32),
        grid=(depth, d3 // col),
        in_specs=[
            pl.BlockSpec((rows, d), lambda l, j: (0, 0)),
            pl.BlockSpec((1, d, col), lambda l, j: (l, 0, j)),
            pl.BlockSpec((1, 1, col), lambda l, j: (l, 0, j)),
        ],
        out_specs=pl.BlockSpec((1, rows, col), lambda l, j: (l, 0, j)),
        name="modulation",
    )(cs, w_mod, b_mod.reshape(depth, 1, d3))


def _head_norm_rope(xt, gain, cos, sin):
    ssq = jnp.sum(xt * xt, axis=1, keepdims=True)
    xn = xt * lax.rsqrt(ssq * (1.0 / HEAD_DIM) + EPS) * gain[None]
    if cos is None:
        return xn
    half = HEAD_DIM // 2
    x1 = xn[:, :half, :]
    x2 = xn[:, half:, :]
    c = cos[None]
    s = sin[None]
    return jnp.concatenate([x1 * c - x2 * s, x2 * c + x1 * s], axis=1)


def _inproj_kernel(*refs, use_rope):
    if use_rope:
        (x_ref, sc_ref, sh_ref, gpre_ref, wtok_ref, wfeat_ref, gq_ref, gk_ref, cos_ref, sin_ref,
         y_ref, sga_ref, sgb_ref, gms_ref, qt_ref, k_ref, vt_ref) = refs
        cos = cos_ref[...]
        sin = sin_ref[...]
    else:
        (x_ref, sc_ref, sh_ref, gpre_ref, wtok_ref, wfeat_ref, gq_ref, gk_ref,
         y_ref, sga_ref, sgb_ref, gms_ref, qt_ref, k_ref, vt_ref) = refs
        cos = sin = None
    d = D_MODEL
    x = x_ref[0]
    ms = jnp.mean(x * x, axis=-1, keepdims=True)
    h = x * lax.rsqrt(ms + EPS) * gpre_ref[...]
    h = h * (1.0 + sc_ref[0]) + sh_ref[0]
    hb = h.astype(BF16)

    def proj(lo, hi):
        return jnp.dot(hb, wtok_ref[:, lo:hi], preferred_element_type=F32)

    a = proj(0, d)
    g = proj(d, 2 * d)
    y_ref[0] = (a * _sigmoid(g)).astype(BF16)
    ga = proj(2 * d, 3 * d)
    sga_ref[0] = (ga * _sigmoid(ga)).astype(BF16)
    gb = proj(3 * d, 4 * d)
    sgb_ref[0] = (gb * _sigmoid(gb)).astype(BF16)
    gm = proj(4 * d, 6 * d)
    gms_ref[0] = _sigmoid(gm).astype(BF16)

    pf = lax.dot_general(wfeat_ref[...], hb, (((1,), (1,)), ((), ())), preferred_element_type=F32)
    t = pf.shape[1]
    q = _head_norm_rope(pf[:d].reshape(N_HEADS, HEAD_DIM, t), gq_ref[...], cos, sin)
    qt_ref[0] = q.reshape(d, t).astype(BF16)
    k = _head_norm_rope(pf[d:d + KV_WIDTH].reshape(N_KV_HEADS, HEAD_DIM, t), gk_ref[...], cos, sin)
    k_ref[0] = k.reshape(KV_WIDTH, t).T.astype(BF16)
    vt_ref[0] = pf[d + KV_WIDTH:].astype(BF16)


def _inproj(x, sc, sh, gpre, wtok, wfeat, gq, gk, cos_t, sin_t):
    b, n, d = x.shape
    t = ROW_TILE
    use_rope = cos_t is not None
    row = lambda bi, i: (bi, i, 0)
    col = lambda bi, i: (bi, 0, i)
    per_batch = lambda bi, i: (bi, 0, 0)
    in_specs = [
        pl.BlockSpec((1, t, d), row),
        pl.BlockSpec((1, 1, d), per_batch),
        pl.BlockSpec((1, 1, d), per_batch),
        _const_spec((1, d)),
        _const_spec(wtok.shape),
        _const_spec(wfeat.shape),
        _const_spec(gq.shape),
        _const_spec(gk.shape),
    ]
    args = [x, sc, sh, gpre, wtok, wfeat, gq, gk]
    if use_rope:
        in_specs += [pl.BlockSpec((ROPE_AXIS_DIM, t), lambda bi, i: (0, i))] * 2
        args += [cos_t, sin_t]
    out_shape = [
        jax.ShapeDtypeStruct((b, n, d), BF16),
        jax.ShapeDtypeStruct((b, n, d), BF16),
        jax.ShapeDtypeStruct((b, n, d), BF16),
        jax.ShapeDtypeStruct((b, n, 2 * d), BF16),
        jax.ShapeDtypeStruct((b, d, n), BF16),
        jax.ShapeDtypeStruct((b, n, KV_WIDTH), BF16),
        jax.ShapeDtypeStruct((b, KV_WIDTH, n), BF16),
    ]
    out_specs = [
        pl.BlockSpec((1, t, d), row),
        pl.BlockSpec((1, t, d), row),
        pl.BlockSpec((1, t, d), row),
        pl.BlockSpec((1, t, 2 * d), row),
        pl.BlockSpec((1, d, t), col),
        pl.BlockSpec((1, t, KV_WIDTH), row),
        pl.BlockSpec((1, KV_WIDTH, t), col),
    ]
    return pl.pallas_call(
        functools.partial(_inproj_kernel, use_rope=use_rope),
        out_shape=out_shape,
        grid=(b, n // t),
        in_specs=in_specs,
        out_specs=out_specs,
        compiler_params=pltpu.CompilerParams(
            dimension_semantics=("arbitrary", "arbitrary"),
            vmem_limit_bytes=V7X_VMEM_LIMIT_BYTES),
        name="inproj_rope" if use_rope else "inproj_ctx",
    )(*args)


def _attn_kernel(qt_ref, k_ref, vt_ref, o_ref, qpad_ref, m_ref, l_ref, acc_ref, *, kv_tile, n_kv):
    tq = qt_ref.shape[2]
    h = pl.program_id(1)
    qpad_ref[...] = jnp.zeros_like(qpad_ref)
    row0 = pl.multiple_of(h * HEAD_DIM, HEAD_DIM)
    for g in range(GROUP):
        qpad_ref[pl.ds(row0, HEAD_DIM), g * tq:(g + 1) * tq] = qt_ref[0, g * HEAD_DIM:(g + 1) * HEAD_DIM, :]
    m_ref[...] = jnp.full_like(m_ref, NEG_BIG)
    l_ref[...] = jnp.zeros_like(l_ref)
    acc_ref[...] = jnp.zeros_like(acc_ref)

    def body(j, carry):
        off = pl.multiple_of(j * kv_tile, kv_tile)
        kblk = k_ref[0, pl.ds(off, kv_tile), :]
        st = jnp.dot(kblk, qpad_ref[...], preferred_element_type=F32)
        m_old = m_ref[...]
        m_new = jnp.maximum(m_old, jnp.max(st, axis=0, keepdims=True))
        alpha = jnp.exp2(m_old - m_new)
        p = jnp.exp2(st - m_new)
        l_ref[...] = alpha * l_ref[...] + jnp.sum(p, axis=0, keepdims=True)
        vblk = vt_ref[0, :, pl.ds(off, kv_tile)]
        acc_ref[...] = alpha * acc_ref[...] + jnp.dot(vblk, p.astype(BF16), preferred_element_type=F32)
        m_ref[...] = m_new
        return carry

    lax.fori_loop(0, n_kv, body, 0)
    out_t = acc_ref[...] / l_ref[...]
    o4 = jnp.concatenate([out_t[:, g * tq:(g + 1) * tq] for g in range(GROUP)], axis=0)
    o_ref[0] = o4.T.astype(BF16)


def _attention(qt, k, vt, kv_tile):
    b, d, n = qt.shape
    skv = k.shape[1]
    tq = ROW_TILE
    gw = GROUP * HEAD_DIM
    n_kv = skv // kv_tile
    return pl.pallas_call(
        functools.partial(_attn_kernel, kv_tile=kv_tile, n_kv=n_kv),
        out_shape=jax.ShapeDtypeStruct((b, n, d), BF16),
        grid=(b, N_KV_HEADS, n // tq),
        in_specs=[
            pl.BlockSpec((1, gw, tq), lambda bi, h, i: (bi, h, i)),
            pl.BlockSpec((1, skv, KV_WIDTH), lambda bi, h, i: (bi, 0, 0)),
            pl.BlockSpec((1, HEAD_DIM, skv), lambda bi, h, i: (bi, h, 0)),
        ],
        out_specs=pl.BlockSpec((1, tq, gw), lambda bi, h, i: (bi, i, h)),
        scratch_shapes=[
            pltpu.VMEM((KV_WIDTH, GROUP * tq), BF16),
            pltpu.VMEM((1, GROUP * tq), F32),
            pltpu.VMEM((1, GROUP * tq), F32),
            pltpu.VMEM((HEAD_DIM, GROUP * tq), F32),
        ],
        compiler_params=pltpu.CompilerParams(
            dimension_semantics=("arbitrary", "arbitrary", "arbitrary"),
            vmem_limit_bytes=V7X_VMEM_LIMIT_BYTES),
        name="attn_kv%d" % skv,
    )(qt, k, vt)


def _outproj_kernel(yp_ref, yc_ref, yn_ref, sga_ref, o_ref, sgb_ref, gms_ref, x_ref, gt_ref,
                    cw_ref, cb_ref, lng_ref, lnb_ref, wco_ref, wao_ref, wo_ref, gpost_ref,
                    out_ref, ybuf_ref, conv_ref):
    i = pl.program_id(1)
    n_tiles = pl.num_programs(1)
    t = yc_ref.shape[1]
    d = D_MODEL
    hr = HALO_ROWS
    prev = yp_ref[0].astype(F32)
    nxt = yn_ref[0].astype(F32)
    ybuf_ref[0:hr, :] = jnp.where(i > 0, prev, 0.0)
    ybuf_ref[hr:hr + t, :] = yc_ref[0].astype(F32)
    ybuf_ref[hr + t:hr + t + hr, :] = jnp.where(i < n_tiles - 1, nxt, 0.0)

    row_chunk = 64
    base = hr - CONV_PAD
    for c in range(d // V7X_LANES):
        lanes = slice(c * V7X_LANES, (c + 1) * V7X_LANES)
        for r in range(t // row_chunk):
            acc = jnp.broadcast_to(cb_ref[:, lanes], (row_chunk, V7X_LANES))
            for kk in range(CONV_KERNEL):
                start = base + r * row_chunk + kk
                acc = acc + cw_ref[kk:kk + 1, lanes] * ybuf_ref[start:start + row_chunk, lanes]
            conv_ref[r * row_chunk:(r + 1) * row_chunk, lanes] = acc

    cv = conv_ref[...]
    mu = jnp.mean(cv, axis=-1, keepdims=True)
    cen = cv - mu
    var = jnp.mean(cen * cen, axis=-1, keepdims=True)
    z = cen * lax.rsqrt(var + EPS) * lng_ref[...] + lnb_ref[...]
    z = z * _sigmoid(z)
    z = z * sga_ref[0].astype(F32)
    y_conv = jnp.dot(z.astype(BF16), wco_ref[...], preferred_element_type=F32)

    og = o_ref[0].astype(F32) * sgb_ref[0].astype(F32)
    y_attn = jnp.dot(og.astype(BF16), wao_ref[...], preferred_element_type=F32)

    merged = gms_ref[0, :, :d].astype(F32) * y_conv + gms_ref[0, :, d:].astype(F32) * y_attn
    out = jnp.dot(merged.astype(BF16), wo_ref[...], preferred_element_type=F32)
    ms = jnp.mean(out * out, axis=-1, keepdims=True)
    normed = out * lax.rsqrt(ms + EPS) * gpost_ref[...]
    out_ref[0] = x_ref[0] + gt_ref[0] * normed


def _outproj(y, sga, o, sgb, gms, x, gt, cw, cb, lng, lnb, wco, wao, wo, gpost):
    b, n, d = x.shape
    t = ROW_TILE
    n_tiles = n // t
    halo_per_tile = t // HALO_ROWS
    n_halo = n // HALO_ROWS
    row = lambda bi, i: (bi, i, 0)
    per_batch = lambda bi, i: (bi, 0, 0)
    prev_map = lambda bi, i: (bi, jnp.maximum(i * halo_per_tile - 1, 0), 0)
    next_map = lambda bi, i: (bi, jnp.minimum((i + 1) * halo_per_tile, n_halo - 1), 0)
    in_specs = [
        pl.BlockSpec((1, HALO_ROWS, d), prev_map),
        pl.BlockSpec((1, t, d), row),
        pl.BlockSpec((1, HALO_ROWS, d), next_map),
        pl.BlockSpec((1, t, d), row),
        pl.BlockSpec((1, t, d), row),
        pl.BlockSpec((1, t, d), row),
        pl.BlockSpec((1, t, 2 * d), row),
        pl.BlockSpec((1, t, d), row),
        pl.BlockSpec((1, 1, d), per_batch),
        _const_spec(cw.shape),
        _const_spec((1, d)),
        _const_spec((1, d)),
        _const_spec((1, d)),
        _const_spec((d, d)),
        _const_spec((d, d)),
        _const_spec((d, d)),
        _const_spec((1, d)),
    ]
    return pl.pallas_call(
        _outproj_kernel,
        out_shape=jax.ShapeDtypeStruct((b, n, d), F32),
        grid=(b, n_tiles),
        in_specs=in_specs,
        out_specs=pl.BlockSpec((1, t, d), row),
        scratch_shapes=[
            pltpu.VMEM((t + 2 * HALO_ROWS, d), F32),
            pltpu.VMEM((t, d), F32),
        ],
        compiler_params=pltpu.CompilerParams(
            dimension_semantics=("arbitrary", "arbitrary"),
            vmem_limit_bytes=V7X_VMEM_LIMIT_BYTES),
        name="outproj_n%d" % n,
    )(y, y, y, sga, o, sgb, gms, x, gt, cw, cb, lng, lnb, wco, wao, wo, gpost)


def _rope_tables_t(n_tokens):
    rows = n_tokens // GRID_W
    row = jnp.repeat(jnp.arange(rows, dtype=F32), GRID_W)
    col = jnp.tile(jnp.arange(GRID_W, dtype=F32), rows)
    inv_freq = ROPE_THETA ** (-jnp.arange(0, ROPE_AXIS_DIM, 2, dtype=F32) / ROPE_AXIS_DIM)
    ang = jnp.concatenate([row[:, None] * inv_freq, col[:, None] * inv_freq], axis=-1)
    return jnp.cos(ang).T, jnp.sin(ang).T


def kernel(x, c, ctx, c_ctx, w_mod, b_mod, g_pre, g_post, w_in, conv_w, conv_b, ln_g, ln_b,
           w_conv_out, q_norm_g, k_norm_g, w_attn_out, w_out):
    b, n, d = x.shape
    depth = w_mod.shape[0]
    n_ctx = ctx.shape[1]
    cos_t, sin_t = _rope_tables_t(n)

    mod_rows = 8
    cs = jnp.concatenate([c, c_ctx[None, :], jnp.zeros((mod_rows - b - 1, d), F32)], axis=0)
    mod = _modulation(cs, w_mod, b_mod)

    kv_tile_lat = 768
    assert (n + n_ctx) % kv_tile_lat == 0 and n % ROW_TILE == 0 and n_ctx % ROW_TILE == 0

    for l in range(depth):
        last = l == depth - 1
        sh, sc, gt = (mod[l, :b, None, k * d:(k + 1) * d] for k in range(3))
        shc, scc, gtc = (jnp.broadcast_to(mod[l, b, k * d:(k + 1) * d], (b, 1, d)) for k in range(3))

        w = w_in[l]
        qkv_lo = 3 * d
        qkv_hi = qkv_lo + d + 2 * KV_WIDTH
        wtok = jnp.concatenate([w[:, :qkv_lo], w[:, qkv_hi:]], axis=1).astype(BF16)
        wfeat = w[:, qkv_lo:qkv_hi].T.astype(BF16)
        gq = jnp.broadcast_to((q_norm_g[l] * (ATTN_SCALE * LOG2E))[:, None], (HEAD_DIM, ROW_TILE))
        gk = jnp.broadcast_to(k_norm_g[l][:, None], (HEAD_DIM, ROW_TILE))
        gpre = g_pre[l][None, :]

        y, sga, sgb, gms, qt, k, vt = _inproj(x, sc, sh, gpre, wtok, wfeat, gq, gk, cos_t, sin_t)
        y_c, sga_c, sgb_c, gms_c, qt_c, k_c, vt_c = _inproj(ctx, scc, shc, gpre, wtok, wfeat, gq, gk,
                                                            None, None)

        k_all = jnp.concatenate([k, k_c], axis=1)
        vt_all = jnp.concatenate([vt, vt_c], axis=2)
        o = _attention(qt, k_all, vt_all, kv_tile_lat)

        tail = (conv_w[l], conv_b[l][None, :], ln_g[l][None, :], ln_b[l][None, :],
                w_conv_out[l].astype(BF16), w_attn_out[l].astype(BF16), w_out[l].astype(BF16),
                g_post[l][None, :])
        x_new = _outproj(y, sga, o, sgb, gms, x, gt, *tail)
        if not last:
            o_c = _attention(qt_c, k_c, vt_c, n_ctx)
            ctx = _outproj(y_c, sga_c, o_c, sgb_c, gms_c, ctx, gtc, *tail)
        x = x_new
    return x
```
